```python
import jax, jax.numpy as jnp
from jax import lax
import numpy as np


D_MODEL = 1024
BATCH = 16
SEQ = 2048
DEPTH = 1

PLE_DIM = 256
W_A = D_MODEL
N_HEADS_A = 4
HEAD_DIM_A = W_A // N_HEADS_A
QKV_BLOCK = 4
CONV_A = 4
CHUNK = 64
W_B = D_MODEL
CONV_B = 31
EPS = 1e-6
D_IN = 2 * W_A + 3 * W_B + 2 * D_MODEL

kernel_name = "hybrid_mlstm_conformer_gated_block"


def rmsnorm(x, g):
    xf = x.astype(jnp.float32)
    y = xf * lax.rsqrt(jnp.mean(xf * xf, axis=-1, keepdims=True) + EPS)
    return (y * g.astype(jnp.float32)).astype(x.dtype)


def _standardize(x):
    xf = x.astype(jnp.float32)
    xc = xf - jnp.mean(xf, axis=-1, keepdims=True)
    return xc * lax.rsqrt(jnp.mean(xc * xc, axis=-1, keepdims=True) + EPS)


def causal_depthwise_conv(x, w, b):
    K, C = w.shape
    y = lax.conv_general_dilated(x, w[:, None, :].astype(x.dtype), window_strides=(1,),
                                 padding=[(K - 1, 0)], dimension_numbers=('NWC', 'WIO', 'NWC'),
                                 feature_group_count=C)
    return y + b.astype(x.dtype)


def blockdiag(x, w):
    G, bi, bo = w.shape
    y = jnp.einsum('bsgi,gio->bsgo', x.reshape(x.shape[:-1] + (G, bi)), w)
    return y.reshape(x.shape[:-1] + (G * bo,))


def mlstm_chunkwise(q, k, v, ig, lf):
    B, H, S, dh = q.shape
    nc = S // CHUNK

    def to_chunks(t):
        return jnp.moveaxis(t.reshape((B, H, nc, CHUNK) + t.shape[3:]), 2, 0)

    mask = jnp.tril(jnp.ones((CHUNK, CHUNK), dtype=bool))

    def step(carry, xs):
        C, n, m = carry
        qc, kc, vc, ic, fc = xs
        b = jnp.cumsum(fc, axis=-1)
        dlog = jnp.where(mask, b[..., :, None] - b[..., None, :] + ic[..., None, :], -jnp.inf)
        m_inter = b + m[..., None]
        m_comb = jnp.maximum(jnp.max(dlog, axis=-1), m_inter)
        s = jnp.einsum('bhjd,bhsd->bhjs', qc, kc) * jnp.exp(dlog - m_comb[..., None])
        inter = jnp.exp(m_inter - m_comb)
        num = jnp.einsum('bhjs,bhse->bhje', s, vc) + inter[..., None] * jnp.einsum('bhjd,bhde->bhje', qc, C)
        den = jnp.sum(s, axis=-1) + inter * jnp.einsum('bhjd,bhd->bhj', qc, n)
        h = num / jnp.maximum(jnp.abs(den), jnp.exp(-m_comb))[..., None]
        bL = b[..., -1]
        wlog = bL[..., None] - b + ic
        m_new = jnp.maximum(bL + m, jnp.max(wlog, axis=-1))
        w = jnp.exp(wlog - m_new[..., None])
        decay = jnp.exp(bL + m - m_new)
        wk = w[..., None] * kc
        C = decay[..., None, None] * C + jnp.einsum('bhsd,bhse->bhde', wk, vc)
        n = decay[..., None] * n + jnp.sum(wk, axis=2)
        return (C, n, m_new), h

    init = (jnp.zeros((B, H, dh, dh), jnp.float32), jnp.zeros((B, H, dh), jnp.float32),
            jnp.zeros((B, H), jnp.float32))
    _, hs = lax.scan(step, init, (to_chunks(q), to_chunks(k), to_chunks(v), to_chunks(ig), to_chunks(lf)))
    return jnp.moveaxis(hs, 0, 2).reshape(B, H, S, dh)


def setup_inputs(seed: int = 0) -> dict:
    key = jax.random.key(seed)
    ks = jax.random.split(key, 26)
    f32 = jnp.float32

    def nrm(k, shape, scale):
        return jax.random.normal(k, shape, f32) * scale

    H = N_HEADS_A
    G = W_A // QKV_BLOCK
    b_if = jnp.concatenate([
        nrm(ks[10], (DEPTH, H), 0.1),
        jnp.broadcast_to(jnp.linspace(3.0, 6.0, H, dtype=f32), (DEPTH, H)) + nrm(ks[11], (DEPTH, H), 0.01)], axis=-1)
    return {
        'x': nrm(ks[0], (BATCH, SEQ, D_MODEL), 1.0),
        'p': nrm(ks[1], (DEPTH, BATCH, SEQ, PLE_DIM), 1.0),
        'norm_g': 1.0 + nrm(ks[2], (DEPTH, D_MODEL), 0.02),
        'w_in': nrm(ks[3], (DEPTH, D_MODEL, D_IN), D_MODEL ** -0.5),
        'conv_a_w': nrm(ks[4], (DEPTH, CONV_A, W_A), CONV_A ** -0.5),
        'conv_a_b': nrm(ks[5], (DEPTH, W_A), 0.02),
        'wq': nrm(ks[6], (DEPTH, G, QKV_BLOCK, QKV_BLOCK), QKV_BLOCK ** -0.5),
        'wk': nrm(ks[7], (DEPTH, G, QKV_BLOCK, QKV_BLOCK), QKV_BLOCK ** -0.5),
        'wv': nrm(ks[8], (DEPTH, G, QKV_BLOCK, QKV_BLOCK), QKV_BLOCK ** -0.5),
        'w_if': nrm(ks[9], (DEPTH, 3 * W_A, 2 * H), (3 * W_A) ** -0.5),
        'b_if': b_if,
        'mh_norm_g': 1.0 + nrm(ks[12], (DEPTH, W_A), 0.02),
        'skip_a': 1.0 + nrm(ks[13], (DEPTH, W_A), 0.02),
        'w_proj_a': nrm(ks[14], (DEPTH, W_A, D_MODEL), W_A ** -0.5),
        'conv_b_w': nrm(ks[15], (DEPTH, CONV_B, W_B), CONV_B ** -0.5),
        'conv_b_b': nrm(ks[16], (DEPTH, W_B), 0.02),
        'ln_b_g': 1.0 + nrm(ks[17], (DEPTH, W_B), 0.02),
        'ln_b_b': nrm(ks[18], (DEPTH, W_B), 0.02),
        'w_pw2': nrm(ks[19], (DEPTH, W_B, D_MODEL), W_B ** -0.5),
        'b_pw2': nrm(ks[20], (DEPTH, D_MODEL), 0.02),
        'w_out': nrm(ks[21], (DEPTH, D_MODEL, D_MODEL), D_MODEL ** -0.5),
        'w_ple': nrm(ks[22], (DEPTH, PLE_DIM, D_MODEL), PLE_DIM ** -0.5),
        'ple_norm_g': 1.0 + nrm(ks[23], (DEPTH, D_MODEL), 0.02),
        'w_ple_gate': nrm(ks[24], (DEPTH, D_MODEL, D_MODEL), D_MODEL ** -0.5),
        'final_g': 1.0 + nrm(ks[25], (DEPTH, D_MODEL), 0.02)[0],
    }


def reference(x, p, norm_g, w_in, conv_a_w, conv_a_b, wq, wk, wv, w_if, b_if, mh_norm_g, skip_a,
              w_proj_a, conv_b_w, conv_b_b, ln_b_g, ln_b_b, w_pw2, b_pw2, w_out, w_ple, ple_norm_g,
              w_ple_gate, final_g):
    Bsz, S, _ = x.shape
    H, dh = N_HEADS_A, HEAD_DIM_A
    split_at = [W_A, 2 * W_A, 2 * W_A + W_B, 2 * W_A + 2 * W_B, 2 * W_A + 3 * W_B]
    for i in range(DEPTH):
        h = rmsnorm(x, norm_g[i])
        proj = h @ w_in[i]
        xa, za, ub, gb, zb, gates = jnp.split(proj, split_at, axis=-1)

        xc = jax.nn.silu(causal_depthwise_conv(xa, conv_a_w[i], conv_a_b[i]))
        q = blockdiag(xc, wq[i])
        k = blockdiag(xc, wk[i])
        v = blockdiag(xa, wv[i])
        gif = (jnp.concatenate([q, k, v], axis=-1) @ w_if[i] + b_if[i]).astype(jnp.float32)
        ig = jnp.transpose(gif[..., :H], (0, 2, 1))
        lf = jnp.transpose(jax.nn.log_sigmoid(gif[..., H:]), (0, 2, 1))

        def heads(t):
            return jnp.transpose(t.reshape(Bsz, S, H, dh), (0, 2, 1, 3)).astype(jnp.float32)

        ha = mlstm_chunkwise(heads(q), heads(k) * (dh ** -0.5), heads(v), ig, lf)
        ha = jnp.transpose(_standardize(ha), (0, 2, 1, 3)).reshape(Bsz, S, W_A)
        ha = (ha * mh_norm_g[i].astype(jnp.float32)).astype(x.dtype)
        ya = ((ha + skip_a[i] * xc) * jax.nn.silu(za)) @ w_proj_a[i]

        ug = ub * jax.nn.sigmoid(gb)
        c = causal_depthwise_conv(ug, conv_b_w[i], conv_b_b[i])
        c = (_standardize(c) * ln_b_g[i].astype(jnp.float32) + ln_b_b[i].astype(jnp.float32)).astype(x.dtype)
        yb = (jax.nn.silu(c) * jax.nn.silu(zb)) @ w_pw2[i] + b_pw2[i]

        g_a, g_b = jnp.split(gates, 2, axis=-1)
        merged = jax.nn.sigmoid(g_a) * ya + jax.nn.sigmoid(g_b) * yb
        x = x + merged @ w_out[i]

        ple_gate = jax.nn.sigmoid(rmsnorm(x, ple_norm_g[i]) @ w_ple_gate[i])
        x = x + (p[i] @ w_ple[i]) * ple_gate
    return rmsnorm(x, final_g)
```

```python
import functools

import jax
import jax.numpy as jnp
from jax import lax
from jax.experimental import pallas as pl
from jax.experimental.pallas import tpu as pltpu

D_MODEL = 1024
N_HEADS = 4
HEAD_DIM = D_MODEL // N_HEADS
QKV_BLOCK = 4
CONV_A = 4
CONV_B = 31
PLE_DIM = 256
EPS = 1e-6
N_PROJ = 7

TILE = 256
HALO_A = 8
HALO_B = 32
GATE_LANES = 128
VMEM_LIMIT_BYTES = 60000 * 1024

(V_NORM_G, V_CONV_A_B, V_MH_G, V_SKIP, V_CONV_B_B, V_LN_G, V_LN_B, V_B_PW2,
 V_PLE_G, V_FINAL_G) = range(10)
N_VEC_ROWS = 16

F32 = jnp.float32
BF16 = jnp.bfloat16


def _sigmoid(v):
    return jax.nn.sigmoid(v)


def _silu(v):
    return v * _sigmoid(v)


def _rms(v, g):
    return v * lax.rsqrt(jnp.mean(v * v, axis=-1, keepdims=True) + EPS) * g


def _standardize(v):
    vc = v - jnp.mean(v, axis=-1, keepdims=True)
    return vc * lax.rsqrt(jnp.mean(vc * vc, axis=-1, keepdims=True) + EPS)


def _dot(a, b):
    return jnp.dot(a, b, preferred_element_type=F32)


def _block_kernel(x_ref, p_ref, w_in_ref, wq_ref, wk_ref, wv_ref, wif_ref, wpa_ref, wpw2_ref,
                  wout_ref, wpg_ref, wple_ref, vec_ref, caw_ref, cbw_ref, bif_ref,
                  o_ref, xa_ext, ug_ext, yain_ref, c_ref, n_ref, m_ref):
    t = pl.program_id(1)

    @pl.when(t == 0)
    def _reset_sequence_state():
        xa_ext[0:HALO_A, :] = jnp.zeros((HALO_A, D_MODEL), F32)
        ug_ext[0:HALO_B, :] = jnp.zeros((HALO_B, D_MODEL), F32)
        c_ref[...] = jnp.zeros_like(c_ref)
        n_ref[...] = jnp.zeros_like(n_ref)
        m_ref[...] = jnp.zeros_like(m_ref)

    def vec(i):
        return vec_ref[i:i + 1, :]

    x = x_ref[...]
    hb = _rms(x, vec(V_NORM_G)).astype(BF16)

    def proj(i):
        return _dot(hb, w_in_ref[:, i * D_MODEL:(i + 1) * D_MODEL])

    xa = proj(0)
    xa_ext[HALO_A:HALO_A + TILE, :] = xa
    acc = vec(V_CONV_A_B)
    for k in range(CONV_A):
        off = HALO_A - (CONV_A - 1) + k
        acc = acc + caw_ref[k:k + 1, :] * xa_ext[off:off + TILE, :]
    xa_ext[0:HALO_A, :] = xa_ext[TILE:TILE + HALO_A, :]
    xc = _silu(acc)
    xcb = xc.astype(BF16)
    xab = xa.astype(BF16)

    q_f, qb, ksf, ksb, vb = [], [], [], [], []
    gif = bif_ref[...]
    for h in range(N_HEADS):
        cols = slice(h * HEAD_DIM, (h + 1) * HEAD_DIM)
        qh = _dot(xcb[:, cols], wq_ref[h])
        kh = _dot(xcb[:, cols], wk_ref[h])
        vh = _dot(xab[:, cols], wv_ref[h])
        qhb, khb, vhb = qh.astype(BF16), kh.astype(BF16), vh.astype(BF16)
        gif = gif + _dot(qhb, wif_ref[h * HEAD_DIM:(h + 1) * HEAD_DIM, :])
        gif = gif + _dot(khb, wif_ref[D_MODEL + h * HEAD_DIM:D_MODEL + (h + 1) * HEAD_DIM, :])
        gif = gif + _dot(vhb, wif_ref[2 * D_MODEL + h * HEAD_DIM:2 * D_MODEL + (h + 1) * HEAD_DIM, :])
        q_f.append(qh)
        qb.append(qhb)
        ksf.append(kh * (HEAD_DIM ** -0.5))
        ksb.append(khb * (HEAD_DIM ** -0.5))
        vb.append(vhb)

    lf = jnp.minimum(gif, 0.0) - jnp.log1p(jnp.exp(-jnp.abs(gif)))
    row = lax.broadcasted_iota(jnp.int32, (TILE, TILE), 0)
    col = lax.broadcasted_iota(jnp.int32, (TILE, TILE), 1)
    causal = col <= row
    b_all = jnp.dot(causal.astype(F32), lf, precision=lax.Precision.HIGHEST,
                    preferred_element_type=F32)
    gif_t = gif.T
    b_t = b_all.T

    sza = _silu(proj(1))
    for h in range(N_HEADS):
        cols = slice(h * HEAD_DIM, (h + 1) * HEAD_DIM)
        ig_row = gif_t[h:h + 1, :]
        b_row = b_t[N_HEADS + h:N_HEADS + h + 1, :]
        ig_col = gif[:, h:h + 1]
        b_col = b_all[:, N_HEADS + h:N_HEADS + h + 1]
        m_prev = m_ref[h, 0:1, 0:1]

        dlog = jnp.where(causal, b_col + (ig_row - b_row), -jnp.inf)
        m_inter = b_col + m_prev
        m_comb = jnp.maximum(jnp.max(dlog, axis=1, keepdims=True), m_inter)
        s = lax.dot_general(qb[h], ksb[h], (((1,), (1,)), ((), ())),
                            preferred_element_type=F32) * jnp.exp(dlog - m_comb)
        inter = jnp.exp(m_inter - m_comb)
        c_old = c_ref[h]
        n_old = n_ref[h, 0:1, :]
        num = _dot(s.astype(BF16), vb[h]) + inter * _dot(qb[h], c_old.astype(BF16))
        den = (jnp.sum(s, axis=1, keepdims=True)
               + inter * jnp.sum(q_f[h] * n_old, axis=1, keepdims=True))
        hh = num * (1.0 / jnp.maximum(jnp.abs(den), jnp.exp(-m_comb)))

        b_last = b_col[TILE - 1:TILE, :]
        wlog = b_last - b_col + ig_col
        m_new = jnp.maximum(b_last + m_prev, jnp.max(wlog, axis=0, keepdims=True))
        decay = jnp.exp(b_last + m_prev - m_new)
        wk = jnp.exp(wlog - m_new) * ksf[h]
        c_ref[h] = decay * c_old + lax.dot_general(
            wk.astype(BF16), vb[h], (((0,), (0,)), ((), ())), preferred_element_type=F32)
        n_ref[h] = jnp.broadcast_to(decay * n_old + jnp.sum(wk, axis=0, keepdims=True),
                                    n_ref.shape[1:])
        m_ref[h] = jnp.broadcast_to(m_new, m_ref.shape[1:])

        ha = _standardize(hh) * vec_ref[V_MH_G:V_MH_G + 1, cols]
        yain_ref[:, cols] = ((ha + vec_ref[V_SKIP:V_SKIP + 1, cols] * xc[:, cols])
                             * sza[:, cols]).astype(BF16)
    ya = _dot(yain_ref[...], wpa_ref[...])

    ug_ext[HALO_B:HALO_B + TILE, :] = proj(2) * _sigmoid(proj(3))
    conv = vec(V_CONV_B_B)
    for k in range(CONV_B):
        off = HALO_B - (CONV_B - 1) + k
        conv = conv + cbw_ref[k:k + 1, :] * ug_ext[off:off + TILE, :]
    ug_ext[0:HALO_B, :] = ug_ext[TILE:TILE + HALO_B, :]
    cn = _standardize(conv) * vec(V_LN_G) + vec(V_LN_B)
    ybin = (_silu(cn) * _silu(proj(4))).astype(BF16)
    yb = _dot(ybin, wpw2_ref[...]) + vec(V_B_PW2)

    merged = _sigmoid(proj(5)) * ya + _sigmoid(proj(6)) * yb
    x1 = x + _dot(merged.astype(BF16), wout_ref[...])

    ple_gate = _sigmoid(_dot(_rms(x1, vec(V_PLE_G)).astype(BF16), wpg_ref[...]))
    x2 = x1 + _dot(p_ref[...].astype(BF16), wple_ref[...]) * ple_gate
    o_ref[...] = _rms(x2, vec(V_FINAL_G))


def _dense_blocks(w):
    per_head = HEAD_DIM // QKV_BLOCK
    w4 = w.reshape(N_HEADS, per_head, QKV_BLOCK, QKV_BLOCK)
    eye = jnp.eye(per_head, dtype=w.dtype)
    dense = w4[:, :, :, None, :] * eye[None, :, None, :, None]
    return dense.reshape(N_HEADS, HEAD_DIM, HEAD_DIM)


def _resident(shape):
    zeros = (0,) * len(shape)
    return pl.BlockSpec(shape, lambda b, t: zeros, pipeline_mode=pl.Buffered(1))


@jax.jit
def kernel(x, p, norm_g, w_in, conv_a_w, conv_a_b, wq, wk, wv, w_if, b_if, mh_norm_g, skip_a,
           w_proj_a, conv_b_w, conv_b_b, ln_b_g, ln_b_b, w_pw2, b_pw2, w_out, w_ple, ple_norm_g,
           w_ple_gate, final_g):
    batch, seq, d = x.shape
    assert d == D_MODEL and seq % TILE == 0 and norm_g.shape[0] == 1

    rows = [norm_g[0], conv_a_b[0], mh_norm_g[0], skip_a[0], conv_b_b[0], ln_b_g[0], ln_b_b[0],
            b_pw2[0], ple_norm_g[0], final_g]
    vecs = jnp.zeros((N_VEC_ROWS, D_MODEL), F32).at[:len(rows)].set(jnp.stack(rows).astype(F32))
    caw = jnp.zeros((8, D_MODEL), F32).at[:CONV_A].set(conv_a_w[0])
    cbw = jnp.zeros((32, D_MODEL), F32).at[:CONV_B].set(conv_b_w[0])
    wif = jnp.zeros((3 * D_MODEL, GATE_LANES), F32).at[:, :2 * N_HEADS].set(w_if[0]).astype(BF16)
    bif = jnp.zeros((1, GATE_LANES), F32).at[0, :2 * N_HEADS].set(b_if[0])

    weights = [w_in[0].astype(BF16), _dense_blocks(wq[0]).astype(BF16),
               _dense_blocks(wk[0]).astype(BF16), _dense_blocks(wv[0]).astype(BF16), wif,
               w_proj_a[0].astype(BF16), w_pw2[0].astype(BF16), w_out[0].astype(BF16),
               w_ple_gate[0].astype(BF16), w_ple[0].astype(BF16), vecs, caw, cbw, bif]

    tile_spec = pl.BlockSpec((None, TILE, D_MODEL), lambda b, t: (b, t, 0))
    in_specs = [tile_spec, pl.BlockSpec((None, TILE, PLE_DIM), lambda b, t: (b, t, 0))]
    in_specs += [_resident(w.shape) for w in weights]

    return pl.pallas_call(
        _block_kernel,
        grid=(batch, seq // TILE),
        in_specs=in_specs,
        out_specs=tile_spec,
        out_shape=jax.ShapeDtypeStruct(x.shape, x.dtype),
        scratch_shapes=[
            pltpu.VMEM((HALO_A + TILE, D_MODEL), F32),
            pltpu.VMEM((HALO_B + TILE, D_MODEL), F32),
            pltpu.VMEM((TILE, D_MODEL), BF16),
            pltpu.VMEM((N_HEADS, HEAD_DIM, HEAD_DIM), F32),
            pltpu.VMEM((N_HEADS, 8, HEAD_DIM), F32),
            pltpu.VMEM((N_HEADS, 8, 128), F32),
        ],
        compiler_params=pltpu.CompilerParams(
            dimension_semantics=("arbitrary", "arbitrary"),
            vmem_limit_bytes=VMEM_LIMIT_BYTES),
        name="hybrid_block",
    )(x, p[0], *weights)
```

```python
import jax
import jax.numpy as jnp
from jax import lax
from jax.experimental import pallas as pl
from jax.experimental.pallas import tpu as pltpu

D_MODEL = 1024
N_HEADS = 4
HEAD_DIM = D_MODEL // N_HEADS
QKV_BLOCK = 4
CONV_A = 4
CONV_B = 31
PLE_DIM = 256
EPS = 1e-6

SUBLANES = 8
TILE = 256
HALO_A = 8
HALO_B = 32
GATE_LANES = 128
VMEM_LIMIT_BYTES = 60000 * 1024

(V_NORM_G, V_CONV_A_B, V_MH_G, V_SKIP, V_CONV_B_B, V_LN_G, V_LN_B, V_B_PW2,
 V_PLE_G, V_FINAL_G) = range(10)
N_VEC_ROWS = 16

F32 = jnp.float32
BF16 = jnp.bfloat16


def _sigmoid(v):
    return 0.5 * jnp.tanh(0.5 * v) + 0.5


def _silu(v):
    hv = 0.5 * v
    return hv * jnp.tanh(hv) + hv


def _causal_conv(ext_ref, w_ref, bias, n_taps, halo, cols):
    base = halo - (n_taps - 1)
    out = bias
    for r in range(SUBLANES):
        taps = [k for k in range(n_taps) if (base + k) % SUBLANES == r]
        if not taps:
            continue
        rows = TILE if r == 0 else TILE + SUBLANES
        part = None
        for k in taps:
            start = base + k - r
            term = w_ref[k:k + 1, cols] * ext_ref[start:start + rows, cols]
            part = term if part is None else part + term
        out = out + (part if r == 0 else part[r:r + TILE, :])
    return out


def _rms(v, g):
    return v * lax.rsqrt(jnp.mean(v * v, axis=-1, keepdims=True) + EPS) * g


def _standardize(v):
    vc = v - jnp.mean(v, axis=-1, keepdims=True)
    return vc * lax.rsqrt(jnp.mean(vc * vc, axis=-1, keepdims=True) + EPS)


def _dot(a, b):
    return jnp.dot(a, b, preferred_element_type=F32)


def _head_cols(h):
    return slice(h * HEAD_DIM, (h + 1) * HEAD_DIM)


def _block_kernel(x_ref, p_ref, w_in_ref, wq_ref, wk_ref, wv_ref, wif_ref, wpa_ref, wpw2_ref,
                  wout_ref, wpg_ref, wple_ref, vec_ref, caw_ref, cbw_ref, bif_ref,
                  o_ref, xa_ext, ug_ext, yain_ref, c_ref, n_ref, m_ref):
    t = pl.program_id(1)

    @pl.when(t == 0)
    def _reset_sequence_state():
        xa_ext[0:HALO_A, :] = jnp.zeros((HALO_A, D_MODEL), F32)
        ug_ext[0:HALO_B, :] = jnp.zeros((HALO_B, D_MODEL), F32)
        c_ref[...] = jnp.zeros_like(c_ref)
        n_ref[...] = jnp.zeros_like(n_ref)
        m_ref[...] = jnp.zeros_like(m_ref)

    def vec(i, cols=slice(None)):
        return vec_ref[i:i + 1, cols]

    x = x_ref[...]
    hb = _rms(x, vec(V_NORM_G)).astype(BF16)

    def proj(i):
        return _dot(hb, w_in_ref[:, i * D_MODEL:(i + 1) * D_MODEL])

    ug_ext[HALO_B:HALO_B + TILE, :] = proj(2) * _sigmoid(proj(3))

    def conv_b(h):
        cols = _head_cols(h)
        return _causal_conv(ug_ext, cbw_ref, vec(V_CONV_B_B, cols), CONV_B, HALO_B, cols)

    xa = proj(0)
    xa_ext[HALO_A:HALO_A + TILE, :] = xa
    xc = _silu(_causal_conv(xa_ext, caw_ref, vec(V_CONV_A_B), CONV_A, HALO_A, slice(None)))
    xa_ext[0:HALO_A, :] = xa_ext[TILE:TILE + HALO_A, :]
    xcb = xc.astype(BF16)
    xab = xa.astype(BF16)
    conv_blocks = [conv_b(0)]

    q_f, qb, ksf, ksb, vb = [], [], [], [], []
    gif = bif_ref[...]
    for h in range(N_HEADS):
        cols = _head_cols(h)
        qh = _dot(xcb[:, cols], wq_ref[h])
        kh = _dot(xcb[:, cols], wk_ref[h])
        vh = _dot(xab[:, cols], wv_ref[h])
        qhb, khb, vhb = qh.astype(BF16), kh.astype(BF16), vh.astype(BF16)
        gif = gif + _dot(qhb, wif_ref[h * HEAD_DIM:(h + 1) * HEAD_DIM, :])
        gif = gif + _dot(khb, wif_ref[D_MODEL + h * HEAD_DIM:D_MODEL + (h + 1) * HEAD_DIM, :])
        gif = gif + _dot(vhb, wif_ref[2 * D_MODEL + h * HEAD_DIM:2 * D_MODEL + (h + 1) * HEAD_DIM, :])
        q_f.append(qh)
        qb.append(qhb)
        ksf.append(kh * (HEAD_DIM ** -0.5))
        ksb.append(khb * (HEAD_DIM ** -0.5))
        vb.append(vhb)
    conv_blocks.append(conv_b(1))

    lf = jnp.minimum(gif, 0.0) - jnp.log1p(jnp.exp(-jnp.abs(gif)))
    row = lax.broadcasted_iota(jnp.int32, (TILE, TILE), 0)
    col = lax.broadcasted_iota(jnp.int32, (TILE, TILE), 1)
    causal = col <= row
    b_all = jnp.dot(causal.astype(F32), lf, precision=lax.Precision.HIGHEST,
                    preferred_element_type=F32)
    gif_t = gif.T
    b_t = b_all.T

    sza = _silu(proj(1))
    for h in range(N_HEADS):
        cols = _head_cols(h)
        ig_row = gif_t[h:h + 1, :]
        b_row = b_t[N_HEADS + h:N_HEADS + h + 1, :]
        ig_col = gif[:, h:h + 1]
        b_col = b_all[:, N_HEADS + h:N_HEADS + h + 1]
        m_prev = m_ref[h, 0:1, 0:1]

        dlog = jnp.where(causal, b_col + (ig_row - b_row), -jnp.inf)
        m_inter = b_col + m_prev
        m_comb = jnp.maximum(jnp.max(dlog, axis=1, keepdims=True), m_inter)
        s = lax.dot_general(qb[h], ksb[h], (((1,), (1,)), ((), ())),
                            preferred_element_type=F32) * jnp.exp(dlog - m_comb)
        inter = jnp.exp(m_inter - m_comb)
        c_old = c_ref[h]
        n_old = n_ref[h, 0:1, :]
        num = _dot(s.astype(BF16), vb[h]) + inter * _dot(qb[h], c_old.astype(BF16))
        den = (jnp.sum(s, axis=1, keepdims=True)
               + inter * jnp.sum(q_f[h] * n_old, axis=1, keepdims=True))
        hh = num * (1.0 / jnp.maximum(jnp.abs(den), jnp.exp(-m_comb)))

        b_last = b_col[TILE - 1:TILE, :]
        wlog = b_last - b_col + ig_col
        m_new = jnp.maximum(b_last + m_prev, jnp.max(wlog, axis=0, keepdims=True))
        decay = jnp.exp(b_last + m_prev - m_new)
        wk = jnp.exp(wlog - m_new) * ksf[h]
        c_ref[h] = decay * c_old + lax.dot_general(
            wk.astype(BF16), vb[h], (((0,), (0,)), ((), ())), preferred_element_type=F32)
        n_ref[h] = jnp.broadcast_to(decay * n_old + jnp.sum(wk, axis=0, keepdims=True),
                                    n_ref.shape[1:])
        m_ref[h] = jnp.broadcast_to(m_new, m_ref.shape[1:])

        ha = _standardize(hh) * vec(V_MH_G, cols)
        yain_ref[:, cols] = ((ha + vec(V_SKIP, cols) * xc[:, cols]) * sza[:, cols]).astype(BF16)
        if h + 2 < N_HEADS:
            conv_blocks.append(conv_b(h + 2))
    ug_ext[0:HALO_B, :] = ug_ext[TILE:TILE + HALO_B, :]
    ya = _dot(yain_ref[...], wpa_ref[...])

    conv = jnp.concatenate(conv_blocks, axis=1)
    cn = _standardize(conv) * vec(V_LN_G) + vec(V_LN_B)
    ybin = (_silu(cn) * _silu(proj(4))).astype(BF16)
    yb = _dot(ybin, wpw2_ref[...]) + vec(V_B_PW2)

    merged = _sigmoid(proj(5)) * ya + _sigmoid(proj(6)) * yb
    x1 = x + _dot(merged.astype(BF16), wout_ref[...])

    ple_gate = _sigmoid(_dot(_rms(x1, vec(V_PLE_G)).astype(BF16), wpg_ref[...]))
    x2 = x1 + _dot(p_ref[...].astype(BF16), wple_ref[...]) * ple_gate
    o_ref[...] = _rms(x2, vec(V_FINAL_G))


def _dense_blocks(w):
    per_head = HEAD_DIM // QKV_BLOCK
    w4 = w.reshape(N_HEADS, per_head, QKV_BLOCK, QKV_BLOCK)
    eye = jnp.eye(per_head, dtype=w.dtype)
    dense = w4[:, :, :, None, :] * eye[None, :, None, :, None]
    return dense.reshape(N_HEADS, HEAD_DIM, HEAD_DIM)


def _resident(shape):
    zeros = (0,) * len(shape)
    return pl.BlockSpec(shape, lambda b, t: zeros, pipeline_mode=pl.Buffered(1))


@jax.jit
def kernel(x, p, norm_g, w_in, conv_a_w, conv_a_b, wq, wk, wv, w_if, b_if, mh_norm_g, skip_a,
           w_proj_a, conv_b_w, conv_b_b, ln_b_g, ln_b_b, w_pw2, b_pw2, w_out, w_ple, ple_norm_g,
           w_ple_gate, final_g):
    batch, seq, d = x.shape
    assert d == D_MODEL and seq % TILE == 0 and norm_g.shape[0] == 1

    rows = [norm_g[0], conv_a_b[0], mh_norm_g[0], skip_a[0], conv_b_b[0], ln_b_g[0], ln_b_b[0],
            b_pw2[0], ple_norm_g[0], final_g]
    vecs = jnp.zeros((N_VEC_ROWS, D_MODEL), F32).at[:len(rows)].set(jnp.stack(rows).astype(F32))
    caw = jnp.zeros((8, D_MODEL), F32).at[:CONV_A].set(conv_a_w[0])
    cbw = jnp.zeros((32, D_MODEL), F32).at[:CONV_B].set(conv_b_w[0])
    wif = jnp.zeros((3 * D_MODEL, GATE_LANES), F32).at[:, :2 * N_HEADS].set(w_if[0]).astype(BF16)
    bif = jnp.zeros((1, GATE_LANES), F32).at[0, :2 * N_HEADS].set(b_if[0])

    weights = [w_in[0].astype(BF16), _dense_blocks(wq[0]).astype(BF16),
               _dense_blocks(wk[0]).astype(BF16), _dense_blocks(wv[0]).astype(BF16), wif,
               w_proj_a[0].astype(BF16), w_pw2[0].astype(BF16), w_out[0].astype(BF16),
               w_ple_gate[0].astype(BF16), w_ple[0].astype(BF16), vecs, caw, cbw, bif]

    tile_spec = pl.BlockSpec((None, TILE, D_MODEL), lambda b, t: (b, t, 0))
    in_specs = [tile_spec, pl.BlockSpec((None, TILE, PLE_DIM), lambda b, t: (b, t, 0))]
    in_specs += [_resident(w.shape) for w in weights]

    return pl.pallas_call(
        _block_kernel,
        grid=(batch, seq // TILE),
        in_specs=in_specs,
        out_specs=tile_spec,
        out_shape=jax.ShapeDtypeStruct(x.shape, x.dtype),
        scratch_shapes=[
            pltpu.VMEM((HALO_A + TILE, D_MODEL), F32),
            pltpu.VMEM((HALO_B + TILE, D_MODEL), F32),
            pltpu.VMEM((TILE, D_MODEL), BF16),
            pltpu.VMEM((N_HEADS, HEAD_DIM, HEAD_DIM), F32),
            pltpu.VMEM((N_HEADS, 8, HEAD_DIM), F32),
            pltpu.VMEM((N_HEADS, 8, 128), F32),
        ],
        compiler_params=pltpu.CompilerParams(
            dimension_semantics=("arbitrary", "arbitrary"),
            vmem_limit_bytes=VMEM_LIMIT_BYTES),
        name="hybrid_block",
    )(x, p[0], *weights)
```

```python
import jax
import jax.numpy as jnp
from jax import lax
from jax.experimental import pallas as pl
from jax.experimental.pallas import tpu as pltpu

D_MODEL = 1024
N_HEADS = 4
HEAD_DIM = D_MODEL // N_HEADS
QKV_BLOCK = 4
CONV_A = 4
CONV_B = 31
PLE_DIM = 256
EPS = 1e-6

SUBLANES = 8
TILE = 256
HALO_A = 8
HALO_B = 32
GATE_LANES = 128
VMEM_LIMIT_BYTES = 60000 * 1024

P_XA, P_ZA, P_UB, P_GB, P_ZB, P_GA, P_GBM = range(7)

(V_NORM_G, V_CONV_A_B, V_MH_G, V_SKIP, V_CONV_B_B, V_LN_G, V_LN_B, V_B_PW2,
 V_PLE_G, V_FINAL_G) = range(10)
N_VEC_ROWS = 16

F32 = jnp.float32
BF16 = jnp.bfloat16


def _sigmoid(v):
    return 0.5 * jnp.tanh(0.5 * v) + 0.5


def _silu(v):
    hv = 0.5 * v
    return hv * jnp.tanh(hv) + hv


def _causal_conv(ext_ref, w_ref, bias, n_taps, halo, cols):
    base = halo - (n_taps - 1)
    out = bias
    for r in range(SUBLANES):
        taps = [k for k in range(n_taps) if (base + k) % SUBLANES == r]
        if not taps:
            continue
        rows = TILE if r == 0 else TILE + SUBLANES
        part = None
        for k in taps:
            start = base + k - r
            term = w_ref[k:k + 1, cols] * ext_ref[start:start + rows, cols]
            part = term if part is None else part + term
        if r:
            part = pltpu.roll(part, rows - r, 0)[0:TILE, :]
        out = out + part
    return out


def _rms(v, g):
    return v * lax.rsqrt(jnp.mean(v * v, axis=-1, keepdims=True) + EPS) * g


def _standardize(v):
    vc = v - jnp.mean(v, axis=-1, keepdims=True)
    return vc * lax.rsqrt(jnp.mean(vc * vc, axis=-1, keepdims=True) + EPS)


def _dot(a, b):
    return jnp.dot(a, b, preferred_element_type=F32)


def _w(packed):
    return pltpu.bitcast(packed, BF16)


def _head_cols(h):
    return slice(h * HEAD_DIM, (h + 1) * HEAD_DIM)


def _block_kernel(x_ref, p_ref, w_in_ref, wq_ref, wk_ref, wv_ref, wif_ref, wpa_ref, wpw2_ref,
                  wout_ref, wpg_ref, wple_ref, vec_ref, caw_ref, cbw_ref, bif_ref,
                  o_ref, xa_ext, ug_ext, yain_ref, c_ref, n_ref, m_ref):
    t = pl.program_id(1)

    @pl.when(t == 0)
    def _reset_sequence_state():
        xa_ext[0:HALO_A, :] = jnp.zeros((HALO_A, D_MODEL), F32)
        ug_ext[0:HALO_B, :] = jnp.zeros((HALO_B, D_MODEL), F32)
        c_ref[...] = jnp.zeros_like(c_ref)
        n_ref[...] = jnp.zeros_like(n_ref)
        m_ref[...] = jnp.zeros_like(m_ref)

    def vec(i, cols=slice(None)):
        return vec_ref[i:i + 1, cols]

    x = x_ref[...]
    hb = _rms(x, vec(V_NORM_G)).astype(BF16)

    def proj(i, cols=slice(0, D_MODEL)):
        lo = i * D_MODEL + cols.start
        return _dot(hb, _w(w_in_ref[:, lo:lo + cols.stop - cols.start]))

    for h in range(N_HEADS):
        cols = _head_cols(h)
        ug_ext[HALO_B:HALO_B + TILE, cols] = proj(P_UB, cols) * _sigmoid(proj(P_GB, cols))

    def conv_b(h):
        cols = _head_cols(h)
        return _causal_conv(ug_ext, cbw_ref, vec(V_CONV_B_B, cols), CONV_B, HALO_B, cols)

    xa = proj(P_XA)
    xa_ext[HALO_A:HALO_A + TILE, :] = xa
    xc = _silu(_causal_conv(xa_ext, caw_ref, vec(V_CONV_A_B), CONV_A, HALO_A, slice(None)))
    xa_ext[0:HALO_A, :] = xa_ext[TILE:TILE + HALO_A, :]
    xcb = xc.astype(BF16)
    xab = xa.astype(BF16)
    conv_blocks = [conv_b(0)]

    q_f, qb, ksf, ksb, vb = [], [], [], [], []
    gif = bif_ref[...]
    half = HEAD_DIM // 2
    for h in range(N_HEADS):
        cols = _head_cols(h)
        qh = _dot(xcb[:, cols], _w(wq_ref[h]))
        kh = _dot(xcb[:, cols], _w(wk_ref[h]))
        vh = _dot(xab[:, cols], _w(wv_ref[h]))
        qhb, khb, vhb = qh.astype(BF16), kh.astype(BF16), vh.astype(BF16)
        for part, operand in enumerate((qhb, khb, vhb)):
            lo = (part * N_HEADS + h) * half
            gif = gif + _dot(operand, _w(wif_ref[lo:lo + half, :]))
        q_f.append(qh)
        qb.append(qhb)
        ksf.append(kh * (HEAD_DIM ** -0.5))
        ksb.append(khb * (HEAD_DIM ** -0.5))
        vb.append(vhb)
    conv_blocks.append(conv_b(1))

    lf = jnp.minimum(gif, 0.0) - jnp.log1p(jnp.exp(-jnp.abs(gif)))
    row = lax.broadcasted_iota(jnp.int32, (TILE, TILE), 0)
    col = lax.broadcasted_iota(jnp.int32, (TILE, TILE), 1)
    causal = col <= row
    b_all = jnp.dot(causal.astype(F32), lf, precision=lax.Precision.HIGHEST,
                    preferred_element_type=F32)
    gif_t = gif.T
    b_t = b_all.T

    sza = _silu(proj(P_ZA))
    deferred = [lambda: _silu(proj(P_ZB)), lambda: _sigmoid(proj(P_GA)),
                lambda: _sigmoid(proj(P_GBM)), lambda: _dot(p_ref[...].astype(BF16), _w(wple_ref[...]))]
    early = []
    for h in range(N_HEADS):
        cols = _head_cols(h)
        ig_row = gif_t[h:h + 1, :]
        b_row = b_t[N_HEADS + h:N_HEADS + h + 1, :]
        ig_col = gif[:, h:h + 1]
        b_col = b_all[:, N_HEADS + h:N_HEADS + h + 1]
        m_prev = m_ref[h, 0:1, 0:1]

        dlog = jnp.where(causal, b_col + (ig_row - b_row), -jnp.inf)
        m_inter = b_col + m_prev
        m_comb = jnp.maximum(jnp.max(dlog, axis=1, keepdims=True), m_inter)
        s = lax.dot_general(qb[h], ksb[h], (((1,), (1,)), ((), ())),
                            preferred_element_type=F32) * jnp.exp(dlog - m_comb)
        inter = jnp.exp(m_inter - m_comb)
        c_old = c_ref[h]
        n_old = n_ref[h, 0:1, :]
        num = _dot(s.astype(BF16), vb[h]) + inter * _dot(qb[h], c_old.astype(BF16))
        den = (jnp.sum(s, axis=1, keepdims=True)
               + inter * jnp.sum(q_f[h] * n_old, axis=1, keepdims=True))
        hh = num * (1.0 / jnp.maximum(jnp.abs(den), jnp.exp(-m_comb)))

        b_last = b_col[TILE - 1:TILE, :]
        wlog = b_last - b_col + ig_col
        m_new = jnp.maximum(b_last + m_prev, jnp.max(wlog, axis=0, keepdims=True))
        decay = jnp.exp(b_last + m_prev - m_new)
        wk = jnp.exp(wlog - m_new) * ksf[h]
        c_ref[h] = decay * c_old + lax.dot_general(
            wk.astype(BF16), vb[h], (((0,), (0,)), ((), ())), preferred_element_type=F32)
        n_ref[h] = jnp.broadcast_to(decay * n_old + jnp.sum(wk, axis=0, keepdims=True),
                                    n_ref.shape[1:])
        m_ref[h] = jnp.broadcast_to(m_new, m_ref.shape[1:])

        ha = _standardize(hh) * vec(V_MH_G, cols)
        yain_ref[:, cols] = ((ha + vec(V_SKIP, cols) * xc[:, cols]) * sza[:, cols]).astype(BF16)
        if h + 2 < N_HEADS:
            conv_blocks.append(conv_b(h + 2))
        early.append(deferred[h]())
    szb, sga, sgb, ple = early
    ug_ext[0:HALO_B, :] = ug_ext[TILE:TILE + HALO_B, :]
    ya = _dot(yain_ref[...], _w(wpa_ref[...]))

    conv = jnp.concatenate(conv_blocks, axis=1)
    cn = _standardize(conv) * vec(V_LN_G) + vec(V_LN_B)
    yb = _dot((_silu(cn) * szb).astype(BF16), _w(wpw2_ref[...])) + vec(V_B_PW2)

    merged = sga * ya + sgb * yb
    x1 = x + _dot(merged.astype(BF16), _w(wout_ref[...]))

    ple_gate = _sigmoid(_dot(_rms(x1, vec(V_PLE_G)).astype(BF16), _w(wpg_ref[...])))
    x2 = x1 + ple * ple_gate
    o_ref[...] = _rms(x2, vec(V_FINAL_G))


def _pack_rows(w):
    bits = lax.bitcast_convert_type(w.astype(BF16), jnp.uint16).astype(jnp.uint32)
    bits = bits.reshape(w.shape[:-2] + (w.shape[-2] // 2, 2, w.shape[-1]))
    return bits[..., 0, :] | (bits[..., 1, :] << 16)


def _dense_blocks(w):
    w_rows = w.reshape(N_HEADS, HEAD_DIM, QKV_BLOCK)
    lane = jnp.arange(HEAD_DIM)
    spread = (lane[None, :] % QKV_BLOCK == jnp.arange(QKV_BLOCK)[:, None]).astype(w.dtype)
    tiled = jnp.einsum('hro,oc->hrc', w_rows, spread, precision=lax.Precision.HIGHEST)
    same_block = lane[:, None] // QKV_BLOCK == lane[None, :] // QKV_BLOCK
    return jnp.where(same_block[None], tiled, 0.0)


def _resident(shape):
    zeros = (0,) * len(shape)
    return pl.BlockSpec(shape, lambda b, t: zeros, pipeline_mode=pl.Buffered(1))


@jax.jit
def kernel(x, p, norm_g, w_in, conv_a_w, conv_a_b, wq, wk, wv, w_if, b_if, mh_norm_g, skip_a,
           w_proj_a, conv_b_w, conv_b_b, ln_b_g, ln_b_b, w_pw2, b_pw2, w_out, w_ple, ple_norm_g,
           w_ple_gate, final_g):
    batch, seq, d = x.shape
    assert d == D_MODEL and seq % TILE == 0 and norm_g.shape[0] == 1

    rows = [norm_g[0], conv_a_b[0], mh_norm_g[0], skip_a[0], conv_b_b[0], ln_b_g[0], ln_b_b[0],
            b_pw2[0], ple_norm_g[0], final_g]
    vecs = jnp.zeros((N_VEC_ROWS, D_MODEL), F32).at[:len(rows)].set(jnp.stack(rows).astype(F32))
    caw = jnp.zeros((8, D_MODEL), F32).at[:CONV_A].set(conv_a_w[0])
    cbw = jnp.zeros((32, D_MODEL), F32).at[:CONV_B].set(conv_b_w[0])
    wif = jnp.zeros((3 * D_MODEL, GATE_LANES), F32).at[:, :2 * N_HEADS].set(w_if[0])
    bif = jnp.zeros((1, GATE_LANES), F32).at[0, :2 * N_HEADS].set(b_if[0])

    matrices = [w_in[0], _dense_blocks(wq[0]), _dense_blocks(wk[0]), _dense_blocks(wv[0]), wif,
                w_proj_a[0], w_pw2[0], w_out[0], w_ple_gate[0], w_ple[0]]
    operands = [_pack_rows(m) for m in matrices] + [vecs, caw, cbw, bif]

    tile_spec = pl.BlockSpec((None, TILE, D_MODEL), lambda b, t: (b, t, 0))
    in_specs = [tile_spec, pl.BlockSpec((None, TILE, PLE_DIM), lambda b, t: (b, t, 0))]
    in_specs += [_resident(w.shape) for w in operands]

    return pl.pallas_call(
        _block_kernel,
        grid=(batch, seq // TILE),
        in_specs=in_specs,
        out_specs=tile_spec,
        out_shape=jax.ShapeDtypeStruct(x.shape, x.dtype),
        scratch_shapes=[
            pltpu.VMEM((HALO_A + TILE, D_MODEL), F32),
            pltpu.VMEM((HALO_B + TILE, D_MODEL), F32),
            pltpu.VMEM((TILE, D_MODEL), BF16),
            pltpu.VMEM((N_HEADS, HEAD_DIM, HEAD_DIM), F32),
            pltpu.VMEM((N_HEADS, 8, HEAD_DIM), F32),
            pltpu.VMEM((N_HEADS, 8, 128), F32),
        ],
        compiler_params=pltpu.CompilerParams(
            dimension_semantics=("arbitrary", "arbitrary"),
            vmem_limit_bytes=VMEM_LIMIT_BYTES),
        name="hybrid_block",
    )(x, p[0], *operands)
```

```python
import jax
import jax.numpy as jnp
from jax import lax
from jax.experimental import pallas as pl
from jax.experimental.pallas import tpu as pltpu

D_MODEL = 1024
N_HEADS = 4
HEAD_DIM = D_MODEL // N_HEADS
QKV_BLOCK = 4
CONV_A = 4
CONV_B = 31
PLE_DIM = 256
EPS = 1e-6

SUBLANES = 8
TILE = 256
HALO_A = 8
HALO_B = 32
GATE_LANES = 128
PACK_BLOCK_COLS = 1024
VMEM_LIMIT_BYTES = 60000 * 1024

P_XA, P_ZA, P_UB, P_GB, P_ZB, P_GA, P_GBM = range(7)

(V_NORM_G, V_CONV_A_B, V_MH_G, V_SKIP, V_CONV_B_B, V_LN_G, V_LN_B, V_B_PW2,
 V_PLE_G, V_FINAL_G) = range(10)
N_VEC_ROWS = 16

F32 = jnp.float32
BF16 = jnp.bfloat16


def _sigmoid(v):
    return 0.5 * jnp.tanh(0.5 * v) + 0.5


def _silu(v):
    hv = 0.5 * v
    return hv * jnp.tanh(hv) + hv


def _shift_rows_up(part, r):
    n_tiles = part.shape[0] // SUBLANES
    tiles = part.reshape(n_tiles, SUBLANES, part.shape[1])
    rot = pltpu.roll(tiles, SUBLANES - r, 1)
    sub = lax.broadcasted_iota(jnp.int32, (n_tiles - 1, SUBLANES, part.shape[1]), 1)
    out = jnp.where(sub < SUBLANES - r, rot[:n_tiles - 1], rot[1:])
    return out.reshape(TILE, part.shape[1])


def _causal_conv(ext_ref, w_ref, bias, n_taps, halo, cols):
    base = halo - (n_taps - 1)
    out = bias
    for r in range(SUBLANES):
        taps = [k for k in range(n_taps) if (base + k) % SUBLANES == r]
        if not taps:
            continue
        rows = TILE if r == 0 else TILE + SUBLANES
        part = None
        for k in taps:
            start = base + k - r
            term = w_ref[k:k + 1, cols] * ext_ref[start:start + rows, cols]
            part = term if part is None else part + term
        out = out + (_shift_rows_up(part, r) if r else part)
    return out


def _rms(v, g):
    return v * lax.rsqrt(jnp.mean(v * v, axis=-1, keepdims=True) + EPS) * g


def _standardize(v):
    vc = v - jnp.mean(v, axis=-1, keepdims=True)
    return vc * lax.rsqrt(jnp.mean(vc * vc, axis=-1, keepdims=True) + EPS)


def _dot(a, b):
    return jnp.dot(a, b, preferred_element_type=F32)


def _w(packed):
    return pltpu.bitcast(packed, BF16)


def _head_cols(h):
    return slice(h * HEAD_DIM, (h + 1) * HEAD_DIM)


def _block_kernel(x_ref, p_ref, w_in_ref, wq_ref, wk_ref, wv_ref, wif_ref, wpa_ref, wpw2_ref,
                  wout_ref, wpg_ref, wple_ref, vec_ref, caw_ref, cbw_ref, bif_ref,
                  o_ref, xa_ext, ug_ext, yain_ref, c_ref, n_ref, m_ref):
    t = pl.program_id(1)

    @pl.when(t == 0)
    def _reset_sequence_state():
        xa_ext[0:HALO_A, :] = jnp.zeros((HALO_A, D_MODEL), F32)
        ug_ext[0:HALO_B, :] = jnp.zeros((HALO_B, D_MODEL), F32)
        c_ref[...] = jnp.zeros_like(c_ref)
        n_ref[...] = jnp.zeros_like(n_ref)
        m_ref[...] = jnp.zeros_like(m_ref)

    def vec(i, cols=slice(None)):
        return vec_ref[i:i + 1, cols]

    x = x_ref[...]
    hb = _rms(x, vec(V_NORM_G)).astype(BF16)

    def proj(i, h):
        lo = i * D_MODEL + h * HEAD_DIM
        return _dot(hb, _w(w_in_ref[:, lo:lo + HEAD_DIM]))

    heads = range(N_HEADS)

    def full(blocks):
        return jnp.concatenate(blocks, axis=1)

    for h in heads:
        ug_ext[HALO_B:HALO_B + TILE, _head_cols(h)] = proj(P_UB, h) * _sigmoid(proj(P_GB, h))

    def conv_b(h):
        cols = _head_cols(h)
        return _causal_conv(ug_ext, cbw_ref, vec(V_CONV_B_B, cols), CONV_B, HALO_B, cols)

    xab = []
    for h in heads:
        xa = proj(P_XA, h)
        xa_ext[HALO_A:HALO_A + TILE, _head_cols(h)] = xa
        xab.append(xa.astype(BF16))
    xc = [_silu(_causal_conv(xa_ext, caw_ref, vec(V_CONV_A_B, _head_cols(h)), CONV_A, HALO_A,
                             _head_cols(h))) for h in heads]
    xa_ext[0:HALO_A, :] = xa_ext[TILE:TILE + HALO_A, :]
    conv_blocks = [conv_b(0)]

    q_f, qb, ksf, ksb, vb = [], [], [], [], []
    gif = bif_ref[...]
    half = HEAD_DIM // 2
    for h in heads:
        xcb = xc[h].astype(BF16)
        qh = _dot(xcb, _w(wq_ref[h]))
        kh = _dot(xcb, _w(wk_ref[h]))
        vh = _dot(xab[h], _w(wv_ref[h]))
        qhb, khb, vhb = qh.astype(BF16), kh.astype(BF16), vh.astype(BF16)
        for part, operand in enumerate((qhb, khb, vhb)):
            lo = (part * N_HEADS + h) * half
            gif = gif + _dot(operand, _w(wif_ref[lo:lo + half, :]))
        q_f.append(qh)
        qb.append(qhb)
        ksf.append(kh * (HEAD_DIM ** -0.5))
        ksb.append(khb * (HEAD_DIM ** -0.5))
        vb.append(vhb)
    conv_blocks.append(conv_b(1))

    lf = jnp.minimum(gif, 0.0) - jnp.log1p(jnp.exp(-jnp.abs(gif)))
    row = lax.broadcasted_iota(jnp.int32, (TILE, TILE), 0)
    col = lax.broadcasted_iota(jnp.int32, (TILE, TILE), 1)
    causal = col <= row
    b_all = jnp.dot(causal.astype(F32), lf, precision=lax.Precision.HIGHEST,
                    preferred_element_type=F32)
    gif_t = gif.T
    b_t = b_all.T

    sza = [_silu(proj(P_ZA, h)) for h in heads]
    deferred = [lambda: full([_silu(proj(P_ZB, h)) for h in heads]),
                lambda: full([_sigmoid(proj(P_GA, h)) for h in heads]),
                lambda: full([_sigmoid(proj(P_GBM, h)) for h in heads]),
                lambda: _dot(p_ref[...].astype(BF16), _w(wple_ref[...]))]
    early = []
    for h in heads:
        cols = _head_cols(h)
        ig_row = gif_t[h:h + 1, :]
        b_row = b_t[N_HEADS + h:N_HEADS + h + 1, :]
        ig_col = gif[:, h:h + 1]
        b_col = b_all[:, N_HEADS + h:N_HEADS + h + 1]
        m_prev = m_ref[h, 0:1, 0:1]

        dlog = jnp.where(causal, b_col + (ig_row - b_row), -jnp.inf)
        m_inter = b_col + m_prev
        m_comb = jnp.maximum(jnp.max(dlog, axis=1, keepdims=True), m_inter)
        s = lax.dot_general(qb[h], ksb[h], (((1,), (1,)), ((), ())),
                            preferred_element_type=F32) * jnp.exp(dlog - m_comb)
        inter = jnp.exp(m_inter - m_comb)
        c_old = c_ref[h]
        n_old = n_ref[h, 0:1, :]
        num = _dot(s.astype(BF16), vb[h]) + inter * _dot(qb[h], c_old.astype(BF16))
        den = (jnp.sum(s, axis=1, keepdims=True)
               + inter * jnp.sum(q_f[h] * n_old, axis=1, keepdims=True))
        hh = num * (1.0 / jnp.maximum(jnp.abs(den), jnp.exp(-m_comb)))

        b_last = b_col[TILE - 1:TILE, :]
        wlog = b_last - b_col + ig_col
        m_new = jnp.maximum(b_last + m_prev, jnp.max(wlog, axis=0, keepdims=True))
        decay = jnp.exp(b_last + m_prev - m_new)
        wk = jnp.exp(wlog - m_new) * ksf[h]
        c_ref[h] = decay * c_old + lax.dot_general(
            wk.astype(BF16), vb[h], (((0,), (0,)), ((), ())), preferred_element_type=F32)
        n_ref[h] = jnp.broadcast_to(decay * n_old + jnp.sum(wk, axis=0, keepdims=True),
                                    n_ref.shape[1:])
        m_ref[h] = jnp.broadcast_to(m_new, m_ref.shape[1:])

        ha = _standardize(hh) * vec(V_MH_G, cols)
        yain_ref[:, cols] = ((ha + vec(V_SKIP, cols) * xc[h]) * sza[h]).astype(BF16)
        if h + 2 < N_HEADS:
            conv_blocks.append(conv_b(h + 2))
        early.append(deferred[h]())
    szb, sga, sgb, ple = early
    ug_ext[0:HALO_B, :] = ug_ext[TILE:TILE + HALO_B, :]
    ya = _dot(yain_ref[...], _w(wpa_ref[...]))

    cn = _standardize(full(conv_blocks)) * vec(V_LN_G) + vec(V_LN_B)
    yb = _dot((_silu(cn) * szb).astype(BF16), _w(wpw2_ref[...])) + vec(V_B_PW2)

    merged = sga * ya + sgb * yb
    x1 = x + _dot(merged.astype(BF16), _w(wout_ref[...]))

    ple_gate = _sigmoid(_dot(_rms(x1, vec(V_PLE_G)).astype(BF16), _w(wpg_ref[...])))
    x2 = x1 + ple * ple_gate
    o_ref[...] = _rms(x2, vec(V_FINAL_G))


def _pack_kernel(w_ref, o_ref):
    o_ref[...] = pltpu.bitcast(w_ref[...].astype(BF16), jnp.uint32)


def _pack_rows(w):
    k, n = w.shape
    bn = min(n, PACK_BLOCK_COLS)
    assert k % (2 * SUBLANES) == 0 and n % bn == 0
    return pl.pallas_call(
        _pack_kernel,
        grid=(n // bn,),
        in_specs=[pl.BlockSpec((k, bn), lambda j: (0, j))],
        out_specs=pl.BlockSpec((k // 2, bn), lambda j: (0, j)),
        out_shape=jax.ShapeDtypeStruct((k // 2, n), jnp.uint32),
        name="pack_weight",
    )(w)


def _dense_blocks(w):
    w_rows = w.reshape(N_HEADS, HEAD_DIM, QKV_BLOCK)
    lane = jnp.arange(HEAD_DIM)
    spread = (lane[None, :] % QKV_BLOCK == jnp.arange(QKV_BLOCK)[:, None]).astype(w.dtype)
    tiled = jnp.einsum('hro,oc->hrc', w_rows, spread, precision=lax.Precision.HIGHEST)
    same_block = lane[:, None] // QKV_BLOCK == lane[None, :] // QKV_BLOCK
    return jnp.where(same_block[None], tiled, 0.0).reshape(N_HEADS * HEAD_DIM, HEAD_DIM)


def _resident(shape):
    zeros = (0,) * len(shape)
    return pl.BlockSpec(shape, lambda b, t: zeros, pipeline_mode=pl.Buffered(1))


@jax.jit
def kernel(x, p, norm_g, w_in, conv_a_w, conv_a_b, wq, wk, wv, w_if, b_if, mh_norm_g, skip_a,
           w_proj_a, conv_b_w, conv_b_b, ln_b_g, ln_b_b, w_pw2, b_pw2, w_out, w_ple, ple_norm_g,
           w_ple_gate, final_g):
    batch, seq, d = x.shape
    assert d == D_MODEL and seq % TILE == 0 and norm_g.shape[0] == 1

    rows = [norm_g[0], conv_a_b[0], mh_norm_g[0], skip_a[0], conv_b_b[0], ln_b_g[0], ln_b_b[0],
            b_pw2[0], ple_norm_g[0], final_g]
    vecs = jnp.zeros((N_VEC_ROWS, D_MODEL), F32).at[:len(rows)].set(jnp.stack(rows).astype(F32))
    caw = jnp.zeros((8, D_MODEL), F32).at[:CONV_A].set(conv_a_w[0])
    cbw = jnp.zeros((32, D_MODEL), F32).at[:CONV_B].set(conv_b_w[0])
    wif = jnp.zeros((3 * D_MODEL, GATE_LANES), F32).at[:, :2 * N_HEADS].set(w_if[0])
    bif = jnp.zeros((1, GATE_LANES), F32).at[0, :2 * N_HEADS].set(b_if[0])

    def packed_blocks(w):
        return _pack_rows(_dense_blocks(w)).reshape(N_HEADS, HEAD_DIM // 2, HEAD_DIM)

    operands = [_pack_rows(w_in[0]), packed_blocks(wq[0]), packed_blocks(wk[0]),
                packed_blocks(wv[0]), _pack_rows(wif), _pack_rows(w_proj_a[0]),
                _pack_rows(w_pw2[0]), _pack_rows(w_out[0]), _pack_rows(w_ple_gate[0]),
                _pack_rows(w_ple[0]), vecs, caw, cbw, bif]

    tile_spec = pl.BlockSpec((None, TILE, D_MODEL), lambda b, t: (b, t, 0))
    in_specs = [tile_spec, pl.BlockSpec((None, TILE, PLE_DIM), lambda b, t: (b, t, 0))]
    in_specs += [_resident(w.shape) for w in operands]

    return pl.pallas_call(
        _block_kernel,
        grid=(batch, seq // TILE),
        in_specs=in_specs,
        out_specs=tile_spec,
        out_shape=jax.ShapeDtypeStruct(x.shape, x.dtype),
        scratch_shapes=[
            pltpu.VMEM((HALO_A + TILE, D_MODEL), F32),
            pltpu.VMEM((HALO_B + TILE, D_MODEL), F32),
            pltpu.VMEM((TILE, D_MODEL), BF16),
            pltpu.VMEM((N_HEADS, HEAD_DIM, HEAD_DIM), F32),
            pltpu.VMEM((N_HEADS, 8, HEAD_DIM), F32),
            pltpu.VMEM((N_HEADS, 8, 128), F32),
        ],
        compiler_params=pltpu.CompilerParams(
            dimension_semantics=("arbitrary", "arbitrary"),
            vmem_limit_bytes=VMEM_LIMIT_BYTES),
        name="hybrid_block",
    )(x, p[0], *operands)
```

```python
import jax
import jax.numpy as jnp
from jax import lax
from jax.experimental import pallas as pl
from jax.experimental.pallas import tpu as pltpu

D_MODEL = 1024
N_HEADS = 4
HEAD_DIM = D_MODEL // N_HEADS
QKV_BLOCK = 4
CONV_A = 4
CONV_B = 31
PLE_DIM = 256
EPS = 1e-6

SUBLANES = 8
TILE = 256
HALO_A = 8
HALO_B = 32
GATE_LANES = 128
PACK_BLOCK_COLS = 1024
VMEM_LIMIT_BYTES = 60000 * 1024

P_XA, P_ZA, P_UB, P_GB, P_ZB, P_GA, P_GBM = range(7)

(V_NORM_G, V_CONV_A_B, V_MH_G, V_SKIP, V_CONV_B_B, V_LN_G, V_LN_B, V_B_PW2,
 V_PLE_G, V_FINAL_G) = range(10)
N_VEC_ROWS = 16

F32 = jnp.float32
BF16 = jnp.bfloat16


def _silu_of_double(hv):
    return hv * jnp.tanh(hv) + hv


def _gate_by_double(a, hv):
    return a * jnp.tanh(hv) + a


def _shift_rows_up(part, r):
    n_tiles = part.shape[0] // SUBLANES
    tiles = part.reshape(n_tiles, SUBLANES, part.shape[1])
    sub = lax.broadcasted_iota(jnp.int32, (n_tiles - 1, SUBLANES, part.shape[1]), 1)
    mixed = jnp.where(sub >= r, tiles[:n_tiles - 1], tiles[1:])
    out = pltpu.roll(mixed, SUBLANES - r, 1)
    return out.reshape(TILE, part.shape[1])


def _causal_conv(ext_ref, w_ref, bias, n_taps, halo, cols):
    base = halo - (n_taps - 1)
    out = bias
    for r in range(SUBLANES):
        taps = [k for k in range(n_taps) if (base + k) % SUBLANES == r]
        if not taps:
            continue
        rows = TILE if r == 0 else TILE + SUBLANES
        part = None
        for k in taps:
            start = base + k - r
            term = w_ref[k:k + 1, cols] * ext_ref[start:start + rows, cols]
            part = term if part is None else part + term
        out = out + (_shift_rows_up(part, r) if r else part)
    return out


def _rms(v, g):
    return v * lax.rsqrt(jnp.mean(v * v, axis=-1, keepdims=True) + EPS) * g


def _standardize(v):
    vc = v - jnp.mean(v, axis=-1, keepdims=True)
    return vc * lax.rsqrt(jnp.mean(vc * vc, axis=-1, keepdims=True) + EPS)


def _dot(a, b):
    return jnp.dot(a, b, preferred_element_type=F32)


def _w(packed):
    return pltpu.bitcast(packed, BF16)


def _head_cols(h):
    return slice(h * HEAD_DIM, (h + 1) * HEAD_DIM)


def _block_kernel(x_ref, p_ref, w_in_ref, wq_ref, wk_ref, wv_ref, wif_ref, wpa_ref, wpw2_ref,
                  wout_ref, wpg_ref, wple_ref, vec_ref, caw_ref, cbw_ref, bif_ref,
                  o_ref, xa_ext, ug_ext, yain_ref, c_ref, n_ref, m_ref):
    t = pl.program_id(1)

    @pl.when(t == 0)
    def _reset_sequence_state():
        xa_ext[0:HALO_A, :] = jnp.zeros((HALO_A, D_MODEL), F32)
        ug_ext[0:HALO_B, :] = jnp.zeros((HALO_B, D_MODEL), F32)
        c_ref[...] = jnp.zeros_like(c_ref)
        n_ref[...] = jnp.zeros_like(n_ref)
        m_ref[...] = jnp.zeros_like(m_ref)

    def vec(i, cols=slice(None)):
        return vec_ref[i:i + 1, cols]

    x = x_ref[...]
    hb = _rms(x, vec(V_NORM_G)).astype(BF16)

    def proj(i, h):
        lo = i * D_MODEL + h * HEAD_DIM
        return _dot(hb, _w(w_in_ref[:, lo:lo + HEAD_DIM]))

    heads = range(N_HEADS)

    def full(blocks):
        return jnp.concatenate(blocks, axis=1)

    for h in heads:
        ug_ext[HALO_B:HALO_B + TILE, _head_cols(h)] = _gate_by_double(proj(P_UB, h), proj(P_GB, h))

    def conv_b(h):
        cols = _head_cols(h)
        return _causal_conv(ug_ext, cbw_ref, vec(V_CONV_B_B, cols), CONV_B, HALO_B, cols)

    xab = []
    for h in heads:
        xa = proj(P_XA, h)
        xa_ext[HALO_A:HALO_A + TILE, _head_cols(h)] = xa
        xab.append(xa.astype(BF16))
    xc = [_silu_of_double(_causal_conv(xa_ext, caw_ref, vec(V_CONV_A_B, _head_cols(h)), CONV_A, HALO_A,
                             _head_cols(h))) for h in heads]
    xa_ext[0:HALO_A, :] = xa_ext[TILE:TILE + HALO_A, :]
    conv_blocks = [conv_b(0)]

    q_f, qb, k_f, kb, vb = [], [], [], [], []
    gif = bif_ref[...]
    half = HEAD_DIM // 2
    for h in heads:
        xcb = xc[h].astype(BF16)
        qh = _dot(xcb, _w(wq_ref[h]))
        kh = _dot(xcb, _w(wk_ref[h]))
        vh = _dot(xab[h], _w(wv_ref[h]))
        qhb, khb, vhb = qh.astype(BF16), kh.astype(BF16), vh.astype(BF16)
        for part, operand in enumerate((qhb, khb, vhb)):
            lo = (part * N_HEADS + h) * half
            gif = gif + _dot(operand, _w(wif_ref[lo:lo + half, :]))
        q_f.append(qh)
        qb.append(qhb)
        k_f.append(kh)
        kb.append(khb)
        vb.append(vhb)
    conv_blocks.append(conv_b(1))

    lf = jnp.minimum(gif, 0.0) - jnp.log1p(jnp.exp(-jnp.abs(gif)))
    row = lax.broadcasted_iota(jnp.int32, (TILE, TILE), 0)
    col = lax.broadcasted_iota(jnp.int32, (TILE, TILE), 1)
    causal = col <= row
    b_all = jnp.dot(causal.astype(F32), lf, precision=lax.Precision.HIGHEST,
                    preferred_element_type=F32)
    gif_t = gif.T
    b_t = b_all.T

    sza = [_silu_of_double(proj(P_ZA, h)) for h in heads]
    deferred = [lambda: full([_silu_of_double(proj(P_ZB, h)) for h in heads]),
                lambda: full([jnp.tanh(proj(P_GA, h)) for h in heads]),
                lambda: full([jnp.tanh(proj(P_GBM, h)) for h in heads]),
                lambda: _dot(p_ref[...].astype(BF16), _w(wple_ref[...]))]
    early = []
    for h in heads:
        cols = _head_cols(h)
        ig_row = gif_t[h:h + 1, :]
        b_row = b_t[N_HEADS + h:N_HEADS + h + 1, :]
        ig_col = gif[:, h:h + 1]
        b_col = b_all[:, N_HEADS + h:N_HEADS + h + 1]
        m_prev = m_ref[h, 0:1, 0:1]

        dlog = jnp.where(causal, b_col + (ig_row - b_row), -jnp.inf)
        m_inter = b_col + m_prev
        m_comb = jnp.maximum(jnp.max(dlog, axis=1, keepdims=True), m_inter)
        s = lax.dot_general(qb[h], kb[h], (((1,), (1,)), ((), ())),
                            preferred_element_type=F32) * jnp.exp(dlog - m_comb)
        inter = jnp.exp(m_inter - m_comb)
        c_old = c_ref[h]
        n_old = n_ref[h, 0:1, :]
        num = _dot(s.astype(BF16), vb[h]) + inter * _dot(qb[h], c_old.astype(BF16))
        den = (jnp.sum(s, axis=1, keepdims=True)
               + inter * jnp.sum(q_f[h] * n_old, axis=1, keepdims=True))
        hh = num * (1.0 / jnp.maximum(jnp.abs(den), jnp.exp(-m_comb)))

        b_last = b_col[TILE - 1:TILE, :]
        wlog = b_last - b_col + ig_col
        m_new = jnp.maximum(b_last + m_prev, jnp.max(wlog, axis=0, keepdims=True))
        decay = jnp.exp(b_last + m_prev - m_new)
        wk = jnp.exp(wlog - m_new) * k_f[h]
        c_ref[h] = decay * c_old + lax.dot_general(
            wk.astype(BF16), vb[h], (((0,), (0,)), ((), ())), preferred_element_type=F32)
        n_ref[h] = jnp.broadcast_to(decay * n_old + jnp.sum(wk, axis=0, keepdims=True),
                                    n_ref.shape[1:])
        m_ref[h] = jnp.broadcast_to(m_new, m_ref.shape[1:])

        ha = _standardize(hh) * vec(V_MH_G, cols)
        yain_ref[:, cols] = ((ha + vec(V_SKIP, cols) * xc[h]) * sza[h]).astype(BF16)
        if h + 2 < N_HEADS:
            conv_blocks.append(conv_b(h + 2))
        early.append(deferred[h]())
    szb, tga, tgb, ple = early
    ug_ext[0:HALO_B, :] = ug_ext[TILE:TILE + HALO_B, :]
    ya = _dot(yain_ref[...], _w(wpa_ref[...]))

    cn = _standardize(full(conv_blocks)) * vec(V_LN_G) + vec(V_LN_B)
    yb = _dot((_silu_of_double(cn) * szb).astype(BF16), _w(wpw2_ref[...])) + vec(V_B_PW2)

    merged = (tga * ya + ya) + (tgb * yb + yb)
    x1 = x + _dot(merged.astype(BF16), _w(wout_ref[...]))

    ple_gate = jnp.tanh(_dot(_rms(x1, vec(V_PLE_G)).astype(BF16), _w(wpg_ref[...])))
    x2 = x1 + (ple * ple_gate + ple)
    o_ref[...] = _rms(x2, vec(V_FINAL_G))


def _pack_kernel(w_ref, scale_ref, o_ref):
    o_ref[...] = pltpu.bitcast((w_ref[...] * scale_ref[...]).astype(BF16), jnp.uint32)


def _pack_rows(w, col_scale=1.0):
    k, n = w.shape
    bn = min(n, PACK_BLOCK_COLS)
    assert k % (2 * SUBLANES) == 0 and n % bn == 0
    scale = jnp.broadcast_to(jnp.asarray(col_scale, F32), (n,)).reshape(1, n)
    return pl.pallas_call(
        _pack_kernel,
        grid=(n // bn,),
        in_specs=[pl.BlockSpec((k, bn), lambda j: (0, j)), pl.BlockSpec((1, bn), lambda j: (0, j))],
        out_specs=pl.BlockSpec((k // 2, bn), lambda j: (0, j)),
        out_shape=jax.ShapeDtypeStruct((k // 2, n), jnp.uint32),
        name="pack_weight",
    )(w, scale)


def _dense_blocks(w):
    w_rows = w.reshape(N_HEADS, HEAD_DIM, QKV_BLOCK)
    lane = jnp.arange(HEAD_DIM)
    spread = (lane[None, :] % QKV_BLOCK == jnp.arange(QKV_BLOCK)[:, None]).astype(w.dtype)
    tiled = jnp.einsum('hro,oc->hrc', w_rows, spread, precision=lax.Precision.HIGHEST)
    same_block = lane[:, None] // QKV_BLOCK == lane[None, :] // QKV_BLOCK
    return jnp.where(same_block[None], tiled, 0.0).reshape(N_HEADS * HEAD_DIM, HEAD_DIM)


def _resident(shape):
    zeros = (0,) * len(shape)
    return pl.BlockSpec(shape, lambda b, t: zeros, pipeline_mode=pl.Buffered(1))


@jax.jit
def kernel(x, p, norm_g, w_in, conv_a_w, conv_a_b, wq, wk, wv, w_if, b_if, mh_norm_g, skip_a,
           w_proj_a, conv_b_w, conv_b_b, ln_b_g, ln_b_b, w_pw2, b_pw2, w_out, w_ple, ple_norm_g,
           w_ple_gate, final_g):
    batch, seq, d = x.shape
    assert d == D_MODEL and seq % TILE == 0 and norm_g.shape[0] == 1

    k_scale = HEAD_DIM ** -0.5
    rows = [norm_g[0], 0.5 * conv_a_b[0], mh_norm_g[0], skip_a[0], conv_b_b[0], 0.5 * ln_b_g[0],
            0.5 * ln_b_b[0], b_pw2[0], ple_norm_g[0], final_g]
    vecs = jnp.zeros((N_VEC_ROWS, D_MODEL), F32).at[:len(rows)].set(jnp.stack(rows).astype(F32))
    caw = jnp.zeros((8, D_MODEL), F32).at[:CONV_A].set(0.5 * conv_a_w[0])
    cbw = jnp.zeros((32, D_MODEL), F32).at[:CONV_B].set(conv_b_w[0])
    wif = jnp.zeros((3 * D_MODEL, GATE_LANES), F32).at[:, :2 * N_HEADS].set(w_if[0])
    wif = wif.at[D_MODEL:2 * D_MODEL].multiply(1.0 / k_scale)
    bif = jnp.zeros((1, GATE_LANES), F32).at[0, :2 * N_HEADS].set(b_if[0])
    in_scale = jnp.where(jnp.arange(w_in.shape[-1]) // D_MODEL == P_XA, 1.0, 0.5)

    def packed_blocks(w, scale=1.0):
        return _pack_rows(_dense_blocks(w), scale).reshape(N_HEADS, HEAD_DIM // 2, HEAD_DIM)

    operands = [_pack_rows(w_in[0], in_scale), packed_blocks(wq[0]), packed_blocks(wk[0], k_scale),
                packed_blocks(wv[0]), _pack_rows(wif), _pack_rows(w_proj_a[0]),
                _pack_rows(w_pw2[0]), _pack_rows(w_out[0], 0.5), _pack_rows(w_ple_gate[0], 0.5),
                _pack_rows(w_ple[0], 0.5), vecs, caw, cbw, bif]

    tile_spec = pl.BlockSpec((None, TILE, D_MODEL), lambda b, t: (b, t, 0))
    in_specs = [tile_spec, pl.BlockSpec((None, TILE, PLE_DIM), lambda b, t: (b, t, 0))]
    in_specs += [_resident(w.shape) for w in operands]

    return pl.pallas_call(
        _block_kernel,
        grid=(batch, seq // TILE),
        in_specs=in_specs,
        out_specs=tile_spec,
        out_shape=jax.ShapeDtypeStruct(x.shape, x.dtype),
        scratch_shapes=[
            pltpu.VMEM((HALO_A + TILE, D_MODEL), F32),
            pltpu.VMEM((HALO_B + TILE, D_MODEL), F32),
            pltpu.VMEM((TILE, D_MODEL), BF16),
            pltpu.VMEM((N_HEADS, HEAD_DIM, HEAD_DIM), F32),
            pltpu.VMEM((N_HEADS, 8, HEAD_DIM), F32),
            pltpu.VMEM((N_HEADS, 8, 128), F32),
        ],
        compiler_params=pltpu.CompilerParams(
            dimension_semantics=("arbitrary", "arbitrary"),
            vmem_limit_bytes=VMEM_LIMIT_BYTES),
        name="hybrid_block",
    )(x, p[0], *operands)
```
